```python
import functools
import jax, jax.numpy as jnp
from jax import lax
import numpy as np

D_MODEL = 1024
BATCH = 4
SEQ = 8192
DEPTH = 1
DEC_BATCH = 32
DEC_SEQ = 4
PAST_LEN = 16384
PAGE_SIZE = 128

R_HEADS = 8
R_HEAD_DIM = 64
R_WIDTH = R_HEADS * R_HEAD_DIM
DECAY_RANK = 64
ICLR_RANK = 64
SHIFT_WIDTH = 3 * R_WIDTH + DECAY_RANK + ICLR_RANK
GN_EPS = 64e-5
A_HEADS = 8
A_KV_HEADS = 2
A_HEAD_DIM = 64
A_WIDTH = A_HEADS * A_HEAD_DIM
A_GROUP = A_HEADS // A_KV_HEADS
KV_WIDTH = A_KV_HEADS * A_HEAD_DIM
IDX_HEADS = 8
IDX_DIM = 32
TOPK_MAX = 256
Q_BLOCK = 128
PLE_DIM = 256
RMS_EPS = 1e-6
SPLITS = (SHIFT_WIDTH, R_WIDTH, A_WIDTH, KV_WIDTH, KV_WIDTH, IDX_HEADS * IDX_DIM, IDX_DIM, IDX_HEADS, A_WIDTH, D_MODEL, D_MODEL)
N_IN_COLS = SHIFT_WIDTH + R_WIDTH + 2 * A_WIDTH + 2 * KV_WIDTH + IDX_HEADS * IDX_DIM + IDX_DIM + IDX_HEADS + 2 * D_MODEL

kernel_name = "rwkv7_dsa_gated_hybrid_step"


def _split(z, sizes):
    out, start = [], 0
    for s in sizes:
        out.append(z[..., start:start + s])
        start += s
    return out


def rms_norm(x, g):
    xf = x.astype(jnp.float32)
    y = xf * lax.rsqrt(jnp.mean(xf * xf, axis=-1, keepdims=True) + RMS_EPS)
    return (y * g.astype(jnp.float32)).astype(x.dtype)


def rwkv7_scan(r, decay, k, v, kk, kka, s0):
    def step(s, inp):
        r_t, w_t, k_t, v_t, kk_t, kka_t = inp
        s_kk = jnp.einsum('bhvk,bhk->bhv', s, kk_t)
        s = s * w_t[:, :, None, :] - s_kk[..., None] * kka_t[:, :, None, :] + v_t[..., None] * k_t[:, :, None, :]
        return s, jnp.einsum('bhvk,bhk->bhv', s, r_t)
    xs = tuple(jnp.swapaxes(t, 0, 1) for t in (r, decay, k, v, kk, kka))
    s_final, ys = lax.scan(step, s0, xs)
    return jnp.swapaxes(ys, 0, 1), s_final


def rwkv7_branch(zs, zg, shift0, s0, mu_shift, w0, w2, a0, a2, k_k, k_a, r_k, lnx_g, lnx_b):
    f32 = jnp.float32
    B, T, _ = zs.shape
    prev = jnp.concatenate([shift0[:, None, :].astype(zs.dtype), zs[:, :-1]], axis=1)
    xs = zs + (prev - zs) * mu_shift
    r, k, v, zw, za = _split(xs, (R_WIDTH, R_WIDTH, R_WIDTH, DECAY_RANK, ICLR_RANK))
    w_log = -jax.nn.softplus(-(w0 + jnp.tanh(zw) @ w2).astype(f32)) - 0.5
    decay = jnp.exp(-jnp.exp(w_log))
    a = jax.nn.sigmoid((a0 + za @ a2).astype(f32))
    hd = lambda t: t.astype(f32).reshape(B, T, R_HEADS, R_HEAD_DIM)
    ph = lambda t: t.astype(f32).reshape(R_HEADS, R_HEAD_DIM)
    r, k, v, decay, a = hd(r), hd(k), hd(v), hd(decay), hd(a)
    kk = k * ph(k_k)
    kk = kk / jnp.maximum(jnp.sqrt(jnp.sum(kk * kk, axis=-1, keepdims=True)), 1e-12)
    k = k * (1.0 + (a - 1.0) * ph(k_a))
    y, s_final = rwkv7_scan(r, decay, k, v, kk, kk * a, s0.astype(f32))
    mean = jnp.mean(y, axis=-1, keepdims=True)
    var = jnp.mean(jnp.square(y - mean), axis=-1, keepdims=True)
    y = (y - mean) * lax.rsqrt(var + GN_EPS) * ph(lnx_g) + ph(lnx_b)
    y = y + jnp.sum(r * k * r_k.astype(f32), axis=-1, keepdims=True) * v
    y = y.reshape(B, T, R_WIDTH).astype(zs.dtype) * jax.nn.silu(zg)
    return y, s_final, zs[:, -1]


def _index_scores(qi, ki, wi):
    dots = jnp.einsum('bqhd,bsd->bqhs', qi.astype(jnp.float32), ki.astype(jnp.float32))
    return jnp.einsum('bqhs,bqh->bqs', jax.nn.relu(dots), wi.astype(jnp.float32))


def _sparse_attend(q, kv_sel, valid):
    B, Q = q.shape[:2]
    qg = q.astype(jnp.float32).reshape(B, Q, A_KV_HEADS, A_GROUP, A_HEAD_DIM)
    ks = kv_sel[:, :, :, 0].astype(jnp.float32)
    vs = kv_sel[:, :, :, 1].astype(jnp.float32)
    s = jnp.einsum('bqkgd,bqnkd->bqkgn', qg, ks) * (A_HEAD_DIM ** -0.5)
    s = jnp.where(valid[:, :, None, None, :], s, -jnp.inf)
    p = jax.nn.softmax(s, axis=-1)
    o = jnp.einsum('bqkgn,bqnkd->bqkgd', p, vs)
    return o.reshape(B, Q, A_WIDTH).astype(q.dtype)


def dsa_prompt(q, kv, qi, ki, wi):
    B, T = q.shape[:2]
    n_sel = min(TOPK_MAX, T // 4)
    key_pos = jnp.arange(T)
    bidx = jnp.arange(B)[:, None, None]

    def block(t0):
        qb = lax.dynamic_slice_in_dim(q, t0, Q_BLOCK, axis=1)
        qib = lax.dynamic_slice_in_dim(qi, t0, Q_BLOCK, axis=1)
        wib = lax.dynamic_slice_in_dim(wi, t0, Q_BLOCK, axis=1)
        q_pos = t0 + jnp.arange(Q_BLOCK)
        scores = _index_scores(qib, ki, wib)
        causal = key_pos[None, :] <= q_pos[:, None]
        _, sel = lax.top_k(jnp.where(causal[None], scores, -jnp.inf), n_sel)
        kv_sel = kv[bidx, sel]
        valid = sel <= q_pos[None, :, None]
        return _sparse_attend(qb, kv_sel, valid)

    out = lax.map(block, jnp.arange(0, T, Q_BLOCK))
    return jnp.moveaxis(out, 0, 1).reshape(B, T, A_WIDTH)


def dsa_sample(q, kv, qi, ki, wi, cache_kv, cache_idx_k, page_table):
    Bd, Tn = q.shape[:2]
    n_pages = page_table.shape[1]
    past = n_pages * PAGE_SIZE
    L = past + Tn
    n_sel = min(TOPK_MAX, L // 4)
    ki_past = cache_idx_k[page_table].reshape(Bd, past, IDX_DIM)
    ki_all = jnp.concatenate([ki_past.astype(ki.dtype), ki], axis=1)
    scores = _index_scores(qi, ki_all, wi)
    q_pos = past + jnp.arange(Tn)
    causal = jnp.arange(L)[None, :] <= q_pos[:, None]
    _, sel = lax.top_k(jnp.where(causal[None], scores, -jnp.inf), n_sel)
    bidx = jnp.arange(Bd)[:, None, None]
    sel_past = jnp.minimum(sel, past - 1)
    phys = page_table[bidx, sel_past // PAGE_SIZE]
    kv_past = cache_kv[phys, sel_past % PAGE_SIZE]
    kv_new = kv[bidx, jnp.clip(sel - past, 0, Tn - 1)]
    kv_sel = jnp.where((sel < past)[..., None, None, None], kv_past.astype(kv.dtype), kv_new)
    valid = sel <= q_pos[None, :, None]
    return _sparse_attend(q, kv_sel, valid)


def hybrid_layer(x, p, shift0, s0, attend, g_pre, w_in, mu_shift, w0, w2, a0, a2, k_k, k_a, r_k,
                 lnx_g, lnx_b, w_out_rwkv, w_out_dsa, w_out, g_post, w_ple, w_ple_gate):
    B, T, _ = x.shape
    h = rms_norm(x, g_pre)
    z = h @ w_in
    zs, zg_r, q, k, v, qi, ki, wi, zg_a, m_r, m_a = _split(z, SPLITS)
    y_r, s_new, shift_new = rwkv7_branch(zs, zg_r, shift0, s0, mu_shift, w0, w2, a0, a2,
                                         k_k, k_a, r_k, lnx_g, lnx_b)
    kv = jnp.stack([k.reshape(B, T, A_KV_HEADS, A_HEAD_DIM), v.reshape(B, T, A_KV_HEADS, A_HEAD_DIM)], axis=2)
    y_a = attend(q.reshape(B, T, A_HEADS, A_HEAD_DIM), kv, qi.reshape(B, T, IDX_HEADS, IDX_DIM), ki,
                 wi * ((IDX_HEADS * IDX_DIM) ** -0.5))
    y_a = y_a * jax.nn.silu(zg_a)
    merged = jax.nn.sigmoid(m_r) * (y_r @ w_out_rwkv) + jax.nn.sigmoid(m_a) * (y_a @ w_out_dsa)
    x = x + rms_norm(merged @ w_out, g_post)
    x = x + (p @ w_ple) * jax.nn.sigmoid(x @ w_ple_gate)
    return x, kv, ki, s_new, shift_new


def setup_inputs(seed: int = 0) -> dict:
    key = jax.random.key(seed)
    ks = jax.random.split(key, 32)
    f32 = jnp.float32
    nrm = lambda i, shape, scale: jax.random.normal(ks[i], shape, f32) * scale
    n_pages = PAST_LEN // PAGE_SIZE
    n_used = DEC_BATCH * n_pages
    n_phys = (n_used * 5) // 4
    perm = jax.random.permutation(ks[0], n_phys)
    page_table = perm[:n_used].reshape(DEC_BATCH, n_pages).astype(jnp.int32)
    return {
        'x_prompt': nrm(1, (BATCH, SEQ, D_MODEL), 1.0),
        'x_sample': nrm(2, (DEC_BATCH, DEC_SEQ, D_MODEL), 1.0),
        'p_prompt': nrm(3, (DEPTH, BATCH, SEQ, PLE_DIM), 1.0),
        'p_sample': nrm(4, (DEPTH, DEC_BATCH, DEC_SEQ, PLE_DIM), 1.0),
        'state_rwkv': nrm(5, (DEPTH, DEC_BATCH, R_HEADS, R_HEAD_DIM, R_HEAD_DIM), 0.5),
        'state_shift': nrm(6, (DEPTH, DEC_BATCH, SHIFT_WIDTH), 1.0),
        'cache_kv': nrm(7, (DEPTH, n_phys, PAGE_SIZE, 2, A_KV_HEADS, A_HEAD_DIM), 1.0),
        'cache_idx_k': nrm(8, (DEPTH, n_phys, PAGE_SIZE, IDX_DIM), 1.0),
        'page_table': page_table,
        'g_pre': 1.0 + nrm(9, (DEPTH, D_MODEL), 0.1),
        'w_in': nrm(10, (DEPTH, D_MODEL, N_IN_COLS), D_MODEL ** -0.5),
        'mu_shift': jax.random.uniform(ks[11], (DEPTH, SHIFT_WIDTH), f32),
        'w0': nrm(12, (DEPTH, R_WIDTH), 0.5),
        'w2': nrm(13, (DEPTH, DECAY_RANK, R_WIDTH), DECAY_RANK ** -0.5),
        'a0': nrm(14, (DEPTH, R_WIDTH), 0.1),
        'a2': nrm(15, (DEPTH, ICLR_RANK, R_WIDTH), ICLR_RANK ** -0.5),
        'k_k': 1.0 + nrm(16, (DEPTH, R_WIDTH), 0.1),
        'k_a': 1.0 + nrm(17, (DEPTH, R_WIDTH), 0.1),
        'r_k': nrm(18, (DEPTH, R_HEADS, R_HEAD_DIM), 0.1),
        'lnx_g': 1.0 + nrm(19, (DEPTH, R_WIDTH), 0.1),
        'lnx_b': nrm(20, (DEPTH, R_WIDTH), 0.01),
        'w_out_rwkv': nrm(21, (DEPTH, R_WIDTH, D_MODEL), R_WIDTH ** -0.5),
        'w_out_dsa': nrm(22, (DEPTH, A_WIDTH, D_MODEL), A_WIDTH ** -0.5),
        'w_out': nrm(23, (DEPTH, D_MODEL, D_MODEL), D_MODEL ** -0.5),
        'g_post': 1.0 + nrm(24, (DEPTH, D_MODEL), 0.1),
        'w_ple': nrm(25, (DEPTH, PLE_DIM, D_MODEL), PLE_DIM ** -0.5),
        'w_ple_gate': nrm(26, (DEPTH, D_MODEL, D_MODEL), D_MODEL ** -0.5),
    }


def reference(x_prompt, x_sample, p_prompt, p_sample, state_rwkv, state_shift, cache_kv, cache_idx_k,
              page_table, g_pre, w_in, mu_shift, w0, w2, a0, a2, k_k, k_a, r_k, lnx_g, lnx_b,
              w_out_rwkv, w_out_dsa, w_out, g_post, w_ple, w_ple_gate):
    yp, ys = x_prompt, x_sample
    bp = x_prompt.shape[0]
    kvp, ikp, srp, shp, kvs, iks, srs, shs = [], [], [], [], [], [], [], []
    for l in range(DEPTH):
        yp, kv_l, ik_l, s_l, sh_l = hybrid_layer(
            yp, p_prompt[l],
            jnp.zeros((bp, SHIFT_WIDTH), x_prompt.dtype),
            jnp.zeros((bp, R_HEADS, R_HEAD_DIM, R_HEAD_DIM), jnp.float32),
            dsa_prompt,
            g_pre[l], w_in[l], mu_shift[l], w0[l], w2[l], a0[l], a2[l], k_k[l], k_a[l], r_k[l],
            lnx_g[l], lnx_b[l], w_out_rwkv[l], w_out_dsa[l], w_out[l], g_post[l], w_ple[l], w_ple_gate[l])
        kvp.append(kv_l); ikp.append(ik_l); srp.append(s_l); shp.append(sh_l)
        attend_s = functools.partial(dsa_sample, cache_kv=cache_kv[l], cache_idx_k=cache_idx_k[l],
                                     page_table=page_table)
        ys, kv_l, ik_l, s_l, sh_l = hybrid_layer(
            ys, p_sample[l], state_shift[l], state_rwkv[l], attend_s,
            g_pre[l], w_in[l], mu_shift[l], w0[l], w2[l], a0[l], a2[l], k_k[l], k_a[l], r_k[l],
            lnx_g[l], lnx_b[l], w_out_rwkv[l], w_out_dsa[l], w_out[l], g_post[l], w_ple[l], w_ple_gate[l])
        kvs.append(kv_l); iks.append(ik_l); srs.append(s_l); shs.append(sh_l)
    return (yp, ys, jnp.stack(kvp), jnp.stack(ikp), jnp.stack(srp), jnp.stack(shp),
            jnp.stack(kvs), jnp.stack(iks), jnp.stack(srs), jnp.stack(shs))
```

```python
import functools

import jax
import jax.numpy as jnp
from jax import lax
from jax.experimental import pallas as pl
from jax.experimental.pallas import tpu as pltpu

F32 = jnp.float32
BF16 = jnp.bfloat16
I32 = jnp.int32
HI = lax.Precision.HIGHEST

D_MODEL = 1024
PAGE = 128
HEADS = 8
HD = 64
RW = HEADS * HD
LORA = 64
SHIFT_W = 3 * RW + 2 * LORA
KVH = 2
GROUP = HEADS // KVH
KVW = KVH * HD
IDX_D = 32
IDXW = HEADS * IDX_D
TOPK = 256
QB = 128
PLE = 256
RMS_EPS = 1e-6
GN_EPS = 64e-5
INT_MIN = -2147483648
NEG = -1e30
VMEM_LIMIT = 56 * 1024 * 1024

_O_ZS, _O_ZGR, _O_Q, _O_K, _O_V, _O_QI, _O_KI, _O_WI, _O_ZGA, _O_MR, _O_MA, _O_END = (
    0, 1664, 2176, 2688, 2816, 2944, 3200, 3232, 3240, 3752, 4776, 5800)
QIW_W = 384

NT = (((1,), (1,)), ((), ()))
TN = (((0,), (0,)), ((), ()))


def _params(sem):
    return pltpu.CompilerParams(dimension_semantics=sem, vmem_limit_bytes=VMEM_LIMIT)


def _sigmoid(x):
    return 1.0 / (1.0 + jnp.exp(-x))


def _seg_sum(x, ones_bd):
    hi = x.astype(BF16)
    lo = (x - hi.astype(F32)).astype(BF16)
    return (jnp.dot(hi, ones_bd, preferred_element_type=F32)
            + jnp.dot(lo, ones_bd, preferred_element_type=F32))


def _in_proj_body(x_ref, g_ref, *refs, n_out):
    w_refs, o_refs = refs[:n_out], refs[n_out:]
    x = x_ref[...]
    ms = jnp.mean(x * x, axis=-1, keepdims=True)
    h = ((x * lax.rsqrt(ms + RMS_EPS)) * g_ref[...]).astype(BF16)
    for w_ref, o_ref in zip(w_refs, o_refs):
        o_ref[...] = jnp.dot(h, w_ref[...], preferred_element_type=F32).astype(o_ref.dtype)


def _in_proj(x2d, g_pre, w_groups, out_dtypes, tm):
    n = x2d.shape[0]
    n_out = len(w_groups)
    in_specs = [pl.BlockSpec((tm, D_MODEL), lambda i: (i, 0)),
                pl.BlockSpec((1, D_MODEL), lambda i: (0, 0))]
    in_specs += [pl.BlockSpec(w.shape, lambda i: (0, 0), pipeline_mode=pl.Buffered(1)) for w in w_groups]
    out_specs = [pl.BlockSpec((tm, w.shape[1]), lambda i: (i, 0)) for w in w_groups]
    out_shape = [jax.ShapeDtypeStruct((n, w.shape[1]), dt) for w, dt in zip(w_groups, out_dtypes)]
    return pl.pallas_call(
        functools.partial(_in_proj_body, n_out=n_out),
        grid=(n // tm,), in_specs=in_specs, out_specs=out_specs, out_shape=out_shape,
        compiler_params=_params(("arbitrary",)), name="in_proj",
    )(x2d, g_pre.reshape(1, D_MODEL), *w_groups)


def _in_proj_weights(w_in):
    wb = w_in.astype(BF16)
    pad = jnp.zeros((D_MODEL, QIW_W - (IDXW + IDX_D + HEADS)), BF16)
    qiw = jnp.concatenate([wb[:, _O_QI:_O_KI], wb[:, _O_KI:_O_WI], wb[:, _O_WI:_O_ZGA], pad], axis=1)
    kirep = jnp.tile(wb[:, _O_KI:_O_WI], (1, HEADS))
    groups = [wb[:, _O_ZS:_O_ZGR], wb[:, _O_ZGR:_O_Q], wb[:, _O_Q:_O_K], wb[:, _O_K:_O_QI],
              wb[:, _O_K:_O_QI], qiw, kirep, wb[:, _O_ZGA:_O_MR], wb[:, _O_MR:_O_MA], wb[:, _O_MA:_O_END]]
    dtypes = [F32, F32, F32, F32, BF16, F32, BF16, F32, F32, F32]
    return groups, dtypes


def _rwkv_prep_body(zs_ref, sh_ref, mu_ref, w0_ref, w2_ref, a0_ref, a2_ref, kk_ref, ka_ref, ones_ref,
                    r_o, lw_o, k_o, v_o, kk_o, kka_o, prev_scr, *, tm, t_valid, t_total):
    t = pl.program_id(1)

    @pl.when(t == 0)
    def _():
        prev_scr[...] = sh_ref[0]

    zs = zs_ref[0]
    row = lax.broadcasted_iota(I32, (tm, 1), 0)
    prev = jnp.where(row == 0, prev_scr[...], pltpu.roll(zs, 1, 0))
    prev_scr[...] = zs[tm - 1:tm, :]
    xs = zs + (prev - zs) * mu_ref[...]
    r = xs[:, 0:RW]
    k = xs[:, RW:2 * RW]
    v = xs[:, 2 * RW:3 * RW]
    zw = xs[:, 3 * RW:3 * RW + LORA]
    za = xs[:, 3 * RW + LORA:SHIFT_W]
    y = -(w0_ref[...] + jnp.dot(jnp.tanh(zw), w2_ref[...], precision=HI, preferred_element_type=F32))
    softplus = jnp.maximum(y, 0.0) + jnp.log(1.0 + jnp.exp(-jnp.abs(y)))
    lw = -jnp.exp(-softplus - 0.5)
    a = _sigmoid(a0_ref[...] + jnp.dot(za, a2_ref[...], precision=HI, preferred_element_type=F32))
    kk = k * kk_ref[...]
    norm = jnp.sqrt(_seg_sum(kk * kk, ones_ref[...]))
    kk = kk / jnp.maximum(norm, 1e-12)
    k2 = k * (1.0 + (a - 1.0) * ka_ref[...])
    kka = kk * a
    if t_valid < t_total:
        ok = (t * tm + row) < t_valid
        lw = jnp.where(ok, lw, 0.0)
        k2 = jnp.where(ok, k2, 0.0)
        v = jnp.where(ok, v, 0.0)
        kk = jnp.where(ok, kk, 0.0)
        kka = jnp.where(ok, kka, 0.0)
    r_o[0] = r
    lw_o[0] = lw
    k_o[0] = k2
    v_o[0] = v
    kk_o[0] = kk
    kka_o[0] = kka


def _rwkv_prep(zs, shift0, mu, w0, w2, a0, a2, k_k, k_a, ones_bd, tm, t_valid):
    b, t_total, _ = zs.shape
    row = lambda a: a.reshape(1, -1)
    vec_spec = lambda n: pl.BlockSpec((1, n), lambda i, j: (0, 0))
    in_specs = [pl.BlockSpec((1, tm, SHIFT_W), lambda i, j: (i, j, 0)),
                pl.BlockSpec((1, 1, SHIFT_W), lambda i, j: (i, 0, 0)),
                vec_spec(SHIFT_W), vec_spec(RW), pl.BlockSpec((LORA, RW), lambda i, j: (0, 0)),
                vec_spec(RW), pl.BlockSpec((LORA, RW), lambda i, j: (0, 0)), vec_spec(RW), vec_spec(RW),
                pl.BlockSpec((RW, RW), lambda i, j: (0, 0))]
    out_specs = [pl.BlockSpec((1, tm, RW), lambda i, j: (i, j, 0))] * 6
    out_shape = [jax.ShapeDtypeStruct((b, t_total, RW), F32)] * 6
    return pl.pallas_call(
        functools.partial(_rwkv_prep_body, tm=tm, t_valid=t_valid, t_total=t_total),
        grid=(b, t_total // tm), in_specs=in_specs, out_specs=out_specs, out_shape=out_shape,
        scratch_shapes=[pltpu.VMEM((1, SHIFT_W), F32)],
        compiler_params=_params(("arbitrary", "arbitrary")), name="rwkv_prep",
    )(zs, shift0.reshape(b, 1, SHIFT_W), row(mu), row(w0), w2, row(a0), a2, row(k_k), row(k_a), ones_bd)


def _gn_bonus(y, r, k, v, rk, g, bias, ones_bd):
    mean = _seg_sum(y, ones_bd) * (1.0 / HD)
    d = y - mean
    var = _seg_sum(d * d, ones_bd) * (1.0 / HD)
    yn = d * lax.rsqrt(var + GN_EPS) * g + bias
    return yn + _seg_sum(r * k * rk, ones_bd) * v


def _rwkv_chunk_body(r_ref, lw_ref, k_ref, v_ref, kk_ref, kka_ref, s0_ref, rk_ref, g_ref, b_ref, ones_ref,
                     y_ref, st_ref, s_scr, y_scr, *, c):
    t = pl.program_id(1)

    @pl.when(t == 0)
    def _():
        s_scr[...] = s0_ref[0]

    r = r_ref[0]
    lw = lw_ref[0]
    k = k_ref[0]
    v = v_ref[0]
    kk = kk_ref[0]
    kka = kka_ref[0]
    ri = lax.broadcasted_iota(I32, (c, c), 0)
    ci = lax.broadcasted_iota(I32, (c, c), 1)
    incl = (ri >= ci).astype(F32)
    strict = (ri > ci).astype(F32)
    eye_c = (ri == ci).astype(F32)
    rh = lax.broadcasted_iota(I32, (HD, HD), 0)
    ch = lax.broadcasted_iota(I32, (HD, HD), 1)
    eye_h = (rh == ch).astype(F32)

    dot = functools.partial(jnp.dot, precision=HI, preferred_element_type=F32)
    cum = dot(incl, lw)
    tot = cum[c - 1:c, :]
    kk_t = kk * jnp.exp(cum - lw)
    r_t = r * jnp.exp(cum)
    inv = jnp.exp(-cum)
    kka_h = kka * inv
    k_h = k * inv
    rem = jnp.exp(tot - cum)
    g_all = kka * rem
    h_all = k * rem
    gc = jnp.exp(tot)

    for h in range(HEADS):
        sl = slice(h * HD, (h + 1) * HD)
        kkt, rt, kkah, kh, vh = kk_t[:, sl], r_t[:, sl], kka_h[:, sl], k_h[:, sl], v[:, sl]
        dnt = functools.partial(lax.dot_general, dimension_numbers=NT, precision=HI, preferred_element_type=F32)
        a_ab = dnt(kkt, kkah) * strict
        a_ak = dnt(kkt, kh) * strict
        a_rb = dnt(rt, kkah) * incl
        a_rk = dnt(rt, kh) * incl
        pw = -a_ab
        tinv = eye_c + pw
        n_sq = max(c.bit_length() - 2, 0)
        for _ in range(n_sq):
            pw = dot(pw, pw)
            tinv = tinv + dot(tinv, pw)
        rhs = jnp.concatenate([dot(a_ak, vh), kkt], axis=1)
        up = -dot(tinv, rhs)
        yq = dot(a_rb, up) + jnp.concatenate([dot(a_rk, vh), rt], axis=1)
        dtn = functools.partial(lax.dot_general, dimension_numbers=TN, precision=HI, preferred_element_type=F32)
        n_mat = dtn(g_all[:, sl], up[:, :HD]) + dtn(h_all[:, sl], vh)
        m_mat = dtn(g_all[:, sl], up[:, HD:]) + eye_h * gc[:, sl]
        s0 = s_scr[h]
        y_scr[:, sl] = yq[:, :HD] + dot(yq[:, HD:], s0)
        s_scr[h] = dot(m_mat, s0) + n_mat

    y_ref[0] = _gn_bonus(y_scr[...], r, k, v, rk_ref[...], g_ref[...], b_ref[...], ones_ref[...])

    @pl.when(t == pl.num_programs(1) - 1)
    def _():
        st_ref[0] = s_scr[...]


def _rwkv_chunk(r, lw, k, v, kk, kka, s0t, r_k, lnx_g, lnx_b, ones_bd, c):
    b, t_total, _ = r.shape
    row = lambda a: a.reshape(1, -1)
    seq = pl.BlockSpec((1, c, RW), lambda i, j: (i, j, 0))
    st = pl.BlockSpec((1, HEADS, HD, HD), lambda i, j: (i, 0, 0, 0))
    vec = pl.BlockSpec((1, RW), lambda i, j: (0, 0))
    return pl.pallas_call(
        functools.partial(_rwkv_chunk_body, c=c),
        grid=(b, t_total // c),
        in_specs=[seq] * 6 + [st, vec, vec, vec, pl.BlockSpec((RW, RW), lambda i, j: (0, 0))],
        out_specs=[seq, st],
        out_shape=[jax.ShapeDtypeStruct((b, t_total, RW), F32), jax.ShapeDtypeStruct((b, HEADS, HD, HD), F32)],
        scratch_shapes=[pltpu.VMEM((HEADS, HD, HD), F32), pltpu.VMEM((c, RW), F32)],
        compiler_params=_params(("arbitrary", "arbitrary")), name="rwkv_chunk",
    )(r, lw, k, v, kk, kka, s0t, row(r_k), row(lnx_g), row(lnx_b), ones_bd)


def _rwkv_steps_body(r_ref, lw_ref, k_ref, v_ref, kk_ref, kka_ref, s0_ref, rk_ref, g_ref, b_ref, ones_ref,
                     y_ref, s_ref, y_scr, *, n_steps, rows):
    r = r_ref[0]
    w = jnp.exp(lw_ref[0])
    k = k_ref[0]
    v = v_ref[0]
    kk = kk_ref[0]
    kka = kka_ref[0]
    rh = lax.broadcasted_iota(I32, (HD, HD), 0)
    ch = lax.broadcasted_iota(I32, (HD, HD), 1)
    eye_h = (rh == ch).astype(F32)
    y_scr[...] = jnp.zeros((rows, RW), F32)
    for h in range(HEADS):
        sl = slice(h * HD, (h + 1) * HD)
        s = s0_ref[0, h]
        for t in range(n_steps):
            row = lambda a: a[t:t + 1, sl]
            s_kk = jnp.sum(s * row(kk), axis=1, keepdims=True)
            vk = jnp.dot(eye_h * row(v), jnp.broadcast_to(row(k), (HD, HD)),
                         precision=HI, preferred_element_type=F32)
            s = s * row(w) - s_kk * row(kka) + vk
            y8 = lax.dot_general(jnp.broadcast_to(row(r), (8, HD)), s, NT, precision=HI,
                                 preferred_element_type=F32)
            y_scr[t:t + 1, sl] = y8[0:1, :]
        s_ref[0, h] = s
    y_ref[0] = _gn_bonus(y_scr[...], r, k, v, rk_ref[...], g_ref[...], b_ref[...], ones_ref[...])


def _rwkv_steps(r, lw, k, v, kk, kka, s0, r_k, lnx_g, lnx_b, ones_bd, n_steps):
    b, rows, _ = r.shape
    row = lambda a: a.reshape(1, -1)
    seq = pl.BlockSpec((1, rows, RW), lambda i: (i, 0, 0))
    st = pl.BlockSpec((1, HEADS, HD, HD), lambda i: (i, 0, 0, 0))
    vec = pl.BlockSpec((1, RW), lambda i: (0, 0))
    return pl.pallas_call(
        functools.partial(_rwkv_steps_body, n_steps=n_steps, rows=rows),
        grid=(b,),
        in_specs=[seq] * 6 + [st, vec, vec, vec, pl.BlockSpec((RW, RW), lambda i: (0, 0))],
        out_specs=[seq, st],
        out_shape=[jax.ShapeDtypeStruct((b, rows, RW), F32), jax.ShapeDtypeStruct((b, HEADS, HD, HD), F32)],
        scratch_shapes=[pltpu.VMEM((rows, RW), F32)],
        compiler_params=_params(("arbitrary",)), name="rwkv_steps",
    )(r, lw, k, v, kk, kka, s0, row(r_k), row(lnx_g), row(lnx_b), ones_bd)


def _score_keys(scores):
    s = jnp.where(scores == 0.0, 0.0, scores)
    bits = pltpu.bitcast(s, I32)
    return bits ^ ((bits >> 31) & 0x7FFFFFFF)


def _fold_lanes(x):
    out = x[:, 0:128]
    for j in range(1, x.shape[1] // 128):
        out = out + x[:, j * 128:(j + 1) * 128]
    return out


def _topk_threshold(load_keys, n_chunks, rows, kc, n_sel, idx_bits):
    def count(pred):
        def body(cidx, acc):
            keys = load_keys(cidx)
            col = cidx * kc + lax.broadcasted_iota(I32, (rows, kc), 1)
            return acc + _fold_lanes(jnp.where(pred(keys, col), 1, 0).astype(I32))
        acc = lax.fori_loop(0, n_chunks, body, jnp.zeros((rows, 128), I32))
        return jnp.sum(acc, axis=1, keepdims=True)

    nonneg = count(lambda keys, col: keys >= 0)
    base = jnp.where(nonneg >= n_sel, 0, INT_MIN).astype(I32) * jnp.ones((rows, 1), I32)

    def value_bit(i, base):
        trial = base | jnp.left_shift(jnp.int32(1), 30 - i)
        cnt = count(lambda keys, col: keys >= trial)
        return jnp.where(cnt >= n_sel, trial, base)

    tau = lax.fori_loop(0, 31, value_bit, base)
    need = n_sel - count(lambda keys, col: keys > tau)

    def index_bit(i, x):
        trial = x | jnp.left_shift(jnp.int32(1), idx_bits - 1 - i)
        cnt = count(lambda keys, col: (keys == tau) & (col < trial))
        return jnp.where(cnt <= need, trial, x)

    x = lax.fori_loop(0, idx_bits, index_bit, jnp.zeros((rows, 1), I32))
    return tau, x


def _pad_heads(q, scale):
    rows = q.shape[0]
    lane = lax.broadcasted_iota(I32, (rows, 128), 1)
    out = []
    for h in range(HEADS):
        slab = q[:, (h // 2) * 128:(h // 2 + 1) * 128] * scale
        kvh = h // GROUP
        if (h % 2) != kvh:
            slab = pltpu.roll(slab, HD, 1)
        out.append(jnp.where((lane // HD) == kvh, slab, 0.0))
    return jnp.concatenate(out, axis=0).astype(BF16)


def _unpad_heads(o, rows):
    lane = lax.broadcasted_iota(I32, (rows, 128), 1)
    slabs = []
    for p in range(HEADS // 2):
        parts = []
        for h in (2 * p, 2 * p + 1):
            oh = o[h * rows:(h + 1) * rows, :]
            if (h % 2) != (h // GROUP):
                oh = pltpu.roll(oh, HD, 1)
            parts.append(oh)
        slabs.append(jnp.where(lane < HD, parts[0], parts[1]))
    return slabs


def _attend_chunk(qp, kc, vc, bias, m_scr, l_scr, acc_scr, rows):
    n = kc.shape[0]
    s = lax.dot_general(qp, kc, NT, preferred_element_type=F32)
    s = (s.reshape(HEADS, rows, n) + bias[None]).reshape(HEADS * rows, n)
    m_old = m_scr[...]
    m_new = jnp.maximum(m_old, jnp.max(s, axis=1, keepdims=True))
    alpha = jnp.exp(m_old - m_new)
    p = jnp.exp(s - m_new)
    l_scr[...] = alpha * l_scr[...] + jnp.sum(p, axis=1, keepdims=True)
    acc_scr[...] = alpha * acc_scr[...] + jnp.dot(p.astype(BF16), vc, preferred_element_type=F32)
    m_scr[...] = m_new


def _dsa_prompt_body(q_ref, qiw_ref, kirep_ref, k01_ref, v01_ref, o_ref, keys_scr, m_scr, l_scr, acc_scr,
                     *, kc, n_sel, idx_bits):
    i = pl.program_id(1)
    t0 = i * QB
    n_chunks = (t0 + QB + kc - 1) // kc

    qiw = qiw_ref[0]
    qi = qiw[:, 0:IDXW]
    lane = lax.broadcasted_iota(I32, (QB, IDXW), 1)
    qm = jnp.concatenate([jnp.where((lane // IDX_D) == h, qi, 0.0) for h in range(HEADS)], axis=0).astype(BF16)
    wi = qiw[:, IDXW + IDX_D:IDXW + IDX_D + HEADS] * (float(IDXW) ** -0.5)
    qpos = t0 + lax.broadcasted_iota(I32, (QB, kc), 0)

    def score_chunk(c, carry):
        off = pl.multiple_of(c * kc, kc)
        d = lax.dot_general(qm, kirep_ref[0, pl.ds(off, kc), :], NT, preferred_element_type=F32)
        sc = jnp.zeros((QB, kc), F32)
        for h in range(HEADS):
            sc = sc + jnp.maximum(d[h * QB:(h + 1) * QB, :], 0.0) * wi[:, h:h + 1]
        col = off + lax.broadcasted_iota(I32, (QB, kc), 1)
        keys_scr[:, pl.ds(off, kc)] = jnp.where(col <= qpos, _score_keys(sc), INT_MIN)
        return carry

    lax.fori_loop(0, n_chunks, score_chunk, 0)

    load_keys = lambda c: keys_scr[:, pl.ds(pl.multiple_of(c * kc, kc), kc)]
    tau, x = _topk_threshold(load_keys, n_chunks, QB, kc, n_sel, idx_bits)

    qp = _pad_heads(q_ref[0], float(HD) ** -0.5)
    m_scr[...] = jnp.full((HEADS * QB, 1), NEG, F32)
    l_scr[...] = jnp.zeros((HEADS * QB, 1), F32)
    acc_scr[...] = jnp.zeros((HEADS * QB, 128), F32)

    def attend_chunk(c, carry):
        off = pl.multiple_of(c * kc, kc)
        keys = keys_scr[:, pl.ds(off, kc)]
        col = off + lax.broadcasted_iota(I32, (QB, kc), 1)
        sel = ((keys > tau) | ((keys == tau) & (col < x))) & (col <= qpos)
        bias = jnp.where(sel, 0.0, NEG)
        _attend_chunk(qp, k01_ref[0, pl.ds(off, kc), :], v01_ref[0, pl.ds(off, kc), :], bias,
                      m_scr, l_scr, acc_scr, QB)
        return carry

    lax.fori_loop(0, n_chunks, attend_chunk, 0)
    o = acc_scr[...] / l_scr[...]
    for p, slab in enumerate(_unpad_heads(o, QB)):
        o_ref[0, :, p * 128:(p + 1) * 128] = slab


def _dsa_prompt(q, qiw, kirep, kvb, kc):
    b, t_total, _ = q.shape
    n_sel = min(TOPK, t_total // 4)
    kc = min(kc, t_total)
    idx_bits = max(t_total.bit_length(), 1)
    return pl.pallas_call(
        functools.partial(_dsa_prompt_body, kc=kc, n_sel=n_sel, idx_bits=idx_bits),
        grid=(b, t_total // QB),
        in_specs=[pl.BlockSpec((1, QB, RW), lambda i, j: (i, j, 0)),
                  pl.BlockSpec((1, QB, QIW_W), lambda i, j: (i, j, 0)),
                  pl.BlockSpec((1, t_total, IDXW), lambda i, j: (i, 0, 0)),
                  pl.BlockSpec((1, t_total, 128), lambda i, j: (i, 0, 0)),
                  pl.BlockSpec((1, t_total, 128), lambda i, j: (i, 0, 1))],
        out_specs=pl.BlockSpec((1, QB, RW), lambda i, j: (i, j, 0)),
        out_shape=jax.ShapeDtypeStruct((b, t_total, RW), F32),
        scratch_shapes=[pltpu.VMEM((QB, t_total), I32), pltpu.VMEM((HEADS * QB, 1), F32),
                        pltpu.VMEM((HEADS * QB, 1), F32), pltpu.VMEM((HEADS * QB, 128), F32)],
        compiler_params=_params(("arbitrary", "arbitrary")), name="dsa_prompt",
    )(q, qiw, kirep, kvb, kvb)


def _out_proj_body(x_ref, yr_ref, zgr_ref, ya_ref, zga_ref, mr_ref, ma_ref, p_ref,
                   wr_ref, wa_ref, wo_ref, gp_ref, wple_ref, wg_ref, o_ref):
    silu = lambda z: z * _sigmoid(z)
    bdot = lambda a, w_ref: jnp.dot(a.astype(BF16), w_ref[...], preferred_element_type=F32)
    yr = yr_ref[...] * silu(zgr_ref[...])
    ya = ya_ref[...] * silu(zga_ref[...])
    merged = _sigmoid(mr_ref[...]) * bdot(yr, wr_ref) + _sigmoid(ma_ref[...]) * bdot(ya, wa_ref)
    o1 = bdot(merged, wo_ref)
    ms = jnp.mean(o1 * o1, axis=-1, keepdims=True)
    x2 = x_ref[...] + (o1 * lax.rsqrt(ms + RMS_EPS)) * gp_ref[...]
    o_ref[...] = x2 + bdot(p_ref[...], wple_ref) * _sigmoid(bdot(x2, wg_ref))


def _out_proj(x2d, yr, zgr, ya, zga, mr, ma, p2d, w_r, w_a, w_o, g_post, w_ple, w_gate, tm):
    n = x2d.shape[0]
    rows = lambda w: pl.BlockSpec((tm, w), lambda i: (i, 0))
    full = lambda a: pl.BlockSpec(a.shape, lambda i: (0, 0))
    ws = [w_r.astype(BF16), w_a.astype(BF16), w_o.astype(BF16), g_post.reshape(1, D_MODEL),
          w_ple.astype(BF16), w_gate.astype(BF16)]
    return pl.pallas_call(
        _out_proj_body, grid=(n // tm,),
        in_specs=[rows(D_MODEL), rows(RW), rows(RW), rows(RW), rows(RW), rows(D_MODEL), rows(D_MODEL), rows(PLE)]
        + [full(w) for w in ws],
        out_specs=rows(D_MODEL), out_shape=jax.ShapeDtypeStruct((n, D_MODEL), F32),
        compiler_params=_params(("arbitrary",)), name="out_proj",
    )(x2d, yr, zgr, ya, zga, mr, ma, p2d, *ws)


def _sample_queries(qiw):
    qi = qiw[:, 0:IDXW]
    qs = jnp.concatenate([qi[:, h * IDX_D:(h + 1) * IDX_D] for h in range(HEADS)], axis=0).astype(BF16)
    wi = qiw[:, IDXW + IDX_D:IDXW + IDX_D + HEADS] * (float(IDXW) ** -0.5)
    return qs, wi


def _index_scores(qs, wi, ki, rows):
    d = lax.dot_general(qs, ki, NT, preferred_element_type=F32)
    sc = jnp.zeros((rows, ki.shape[0]), F32)
    for h in range(HEADS):
        sc = sc + jnp.maximum(d[h * rows:(h + 1) * rows, :], 0.0) * wi[:, h:h + 1]
    return sc


def _dsa_scores_body(pt_ref, qiw_ref, *refs, pp, rows, n_new):
    page_refs, (kp_ref, kn_ref) = refs[:pp], refs[pp:]
    j = pl.program_id(1)
    qiw = qiw_ref[0]
    qs, wi = _sample_queries(qiw)
    for p in range(pp):
        sc = _index_scores(qs, wi, page_refs[p][0].astype(BF16), rows)
        kp_ref[0, :, p * PAGE:(p + 1) * PAGE] = _score_keys(sc)

    @pl.when(j == 0)
    def _():
        ki_new = qiw[:, IDXW:IDXW + IDX_D].astype(BF16)
        ki_pad = jnp.concatenate([ki_new, jnp.zeros((PAGE - rows, IDX_D), BF16)], axis=0)
        sc = _index_scores(qs, wi, ki_pad, rows)
        qrow = lax.broadcasted_iota(I32, (rows, PAGE), 0)
        col = lax.broadcasted_iota(I32, (rows, PAGE), 1)
        kn_ref[0] = jnp.where((col <= qrow) & (col < n_new), _score_keys(sc), INT_MIN)


def _dsa_scores(page_table, qiw, cache_idx, pp, n_new):
    b, rows, _ = qiw.shape
    n_pages = page_table.shape[1]
    page_spec = lambda p: pl.BlockSpec((1, PAGE, IDX_D), lambda i, j, pt: (pt[i, j * pp + p], 0, 0))
    grid_spec = pltpu.PrefetchScalarGridSpec(
        num_scalar_prefetch=1, grid=(b, n_pages // pp),
        in_specs=[pl.BlockSpec((1, rows, QIW_W), lambda i, j, pt: (i, 0, 0))] + [page_spec(p) for p in range(pp)],
        out_specs=[pl.BlockSpec((1, rows, pp * PAGE), lambda i, j, pt: (i, 0, j)),
                   pl.BlockSpec((1, rows, PAGE), lambda i, j, pt: (i, 0, 0))])
    return pl.pallas_call(
        functools.partial(_dsa_scores_body, pp=pp, rows=rows, n_new=n_new),
        grid_spec=grid_spec,
        out_shape=[jax.ShapeDtypeStruct((b, rows, n_pages * PAGE), I32), jax.ShapeDtypeStruct((b, rows, PAGE), I32)],
        compiler_params=_params(("arbitrary", "arbitrary")), name="dsa_scores",
    )(page_table, qiw, *([cache_idx] * pp))


def _dsa_sample_body(pt_ref, q_ref, kp_ref, kn_ref, kvn_ref, *refs, pp, rows, n_new, past, kc, l_pad, n_sel,
                     idx_bits):
    page_refs = refs[:pp]
    o_ref, keys_scr, tau_scr, x_scr, m_scr, l_scr, acc_scr = refs[pp:]
    j = pl.program_id(1)

    @pl.when(j == 0)
    def _():
        keys_scr[:, 0:past] = kp_ref[0]
        keys_scr[:, past:past + PAGE] = kn_ref[0]
        if l_pad > past + PAGE:
            keys_scr[:, past + PAGE:l_pad] = jnp.full((rows, l_pad - past - PAGE), INT_MIN, I32)
        load_keys = lambda c: keys_scr[:, pl.ds(pl.multiple_of(c * kc, kc), kc)]
        tau, x = _topk_threshold(load_keys, l_pad // kc, rows, kc, n_sel, idx_bits)
        tau_scr[...] = tau
        x_scr[...] = x
        m_scr[...] = jnp.full((HEADS * rows, 1), NEG, F32)
        l_scr[...] = jnp.zeros((HEADS * rows, 1), F32)
        acc_scr[...] = jnp.zeros((HEADS * rows, 128), F32)

    tau = tau_scr[...]
    x = x_scr[...]
    qp = _pad_heads(q_ref[0], float(HD) ** -0.5)

    def bias_for(off, extra=None):
        keys = keys_scr[:, pl.ds(off, PAGE)]
        col = off + lax.broadcasted_iota(I32, (rows, PAGE), 1)
        sel = (keys > tau) | ((keys == tau) & (col < x))
        if extra is not None:
            sel = sel & extra
        return jnp.where(sel, 0.0, NEG)

    for p in range(pp):
        off = pl.multiple_of((j * pp + p) * PAGE, PAGE)
        kvp = page_refs[p][0]
        _attend_chunk(qp, kvp[:, 0:128].astype(BF16), kvp[:, 128:256].astype(BF16), bias_for(off),
                      m_scr, l_scr, acc_scr, rows)

    @pl.when(j == pl.num_programs(1) - 1)
    def _():
        kvn = jnp.concatenate([kvn_ref[0], jnp.zeros((PAGE - rows, 2 * KVW), F32)], axis=0)
        qrow = lax.broadcasted_iota(I32, (rows, PAGE), 0)
        col = lax.broadcasted_iota(I32, (rows, PAGE), 1)
        _attend_chunk(qp, kvn[:, 0:128].astype(BF16), kvn[:, 128:256].astype(BF16),
                      bias_for(past, (col <= qrow) & (col < n_new)), m_scr, l_scr, acc_scr, rows)
        o = acc_scr[...] / l_scr[...]
        for p, slab in enumerate(_unpad_heads(o, rows)):
            o_ref[0, :, p * 128:(p + 1) * 128] = slab


def _dsa_sample(page_table, q, keys_past, keys_new, kv_new, cache_kv2, pp, n_new, kc):
    b, rows, _ = q.shape
    n_pages = page_table.shape[1]
    past = n_pages * PAGE
    n_sel = min(TOPK, (past + n_new) // 4)
    l_pad = -(-(past + PAGE) // kc) * kc
    idx_bits = l_pad.bit_length()
    page_spec = lambda p: pl.BlockSpec((1, PAGE, 2 * KVW), lambda i, j, pt: (pt[i, j * pp + p], 0, 0))
    per_b = lambda w: pl.BlockSpec((1, rows, w), lambda i, j, pt: (i, 0, 0))
    grid_spec = pltpu.PrefetchScalarGridSpec(
        num_scalar_prefetch=1, grid=(b, n_pages // pp),
        in_specs=[per_b(RW), per_b(past), per_b(PAGE), per_b(2 * KVW)] + [page_spec(p) for p in range(pp)],
        out_specs=per_b(RW),
        scratch_shapes=[pltpu.VMEM((rows, l_pad), I32), pltpu.VMEM((rows, 1), I32), pltpu.VMEM((rows, 1), I32),
                        pltpu.VMEM((HEADS * rows, 1), F32), pltpu.VMEM((HEADS * rows, 1), F32),
                        pltpu.VMEM((HEADS * rows, 128), F32)])
    return pl.pallas_call(
        functools.partial(_dsa_sample_body, pp=pp, rows=rows, n_new=n_new, past=past, kc=kc, l_pad=l_pad,
                          n_sel=n_sel, idx_bits=idx_bits),
        grid_spec=grid_spec, out_shape=jax.ShapeDtypeStruct((b, rows, RW), F32),
        compiler_params=_params(("arbitrary", "arbitrary")), name="dsa_sample",
    )(page_table, q, keys_past, keys_new, kv_new, *([cache_kv2] * pp))


def _ones_block_diag():
    i = jnp.arange(RW) // HD
    return (i[:, None] == i[None, :]).astype(BF16)


def _pad_rows(a, rows):
    return jnp.pad(a, ((0, 0), (0, rows - a.shape[1]), (0, 0)))


def _layer(x, p, shift0, s0, paged, wts):
    (g_pre, w_in, mu_shift, w0, w2, a0, a2, k_k, k_a, r_k, lnx_g, lnx_b,
     w_out_rwkv, w_out_dsa, w_out, g_post, w_ple, w_ple_gate) = wts
    b, t, _ = x.shape
    n = b * t
    x2d = x.reshape(n, D_MODEL)
    ones_bd = _ones_block_diag()
    groups, dtypes = _in_proj_weights(w_in)
    tm = min(512, n)
    zs, zgr, q, kv, kvb, qiw, kirep, zga, mr, ma = _in_proj(x2d, g_pre, groups, dtypes, tm)
    seq = lambda a: a.reshape(b, t, a.shape[-1])
    zs3 = seq(zs)
    rk_flat = r_k.reshape(RW)
    if paged is None:
        prep = _rwkv_prep(zs3, shift0, mu_shift, w0, w2, a0, a2, k_k, k_a, ones_bd, min(512, t), t)
        c = min(64, t)
        y_r, st = _rwkv_chunk(*prep, jnp.swapaxes(s0, -1, -2), rk_flat, lnx_g, lnx_b, ones_bd, c)
        s_new = jnp.swapaxes(st, -1, -2)
        y_a = _dsa_prompt(seq(q), seq(qiw), seq(kirep), seq(kvb), 512)
    else:
        cache_kv2, cache_idx, page_table = paged
        rows = 8
        prep = _rwkv_prep(_pad_rows(zs3, rows), shift0, mu_shift, w0, w2, a0, a2, k_k, k_a, ones_bd, rows, t)
        y_r, s_new = _rwkv_steps(*prep, s0, rk_flat, lnx_g, lnx_b, ones_bd, t)
        y_r = y_r[:, :t]
        qiw_p = _pad_rows(seq(qiw), rows)
        keys_past, keys_new = _dsa_scores(page_table, qiw_p, cache_idx, 8, t)
        y_a = _dsa_sample(page_table, _pad_rows(seq(q), rows), keys_past, keys_new, _pad_rows(seq(kv), rows),
                          cache_kv2, 8, t, 2048)[:, :t]
    y = _out_proj(x2d, y_r.reshape(n, RW), zgr, y_a.reshape(n, RW), zga, mr, ma, p.reshape(n, PLE),
                  w_out_rwkv, w_out_dsa, w_out, g_post, w_ple, w_ple_gate, tm)
    kv_out = kv.reshape(b, t, 2, KVH, HD)
    idx_k = seq(qiw)[:, :, IDXW:IDXW + IDX_D]
    return y.reshape(b, t, D_MODEL), kv_out, idx_k, s_new, zs3[:, -1]


def kernel(x_prompt, x_sample, p_prompt, p_sample, state_rwkv, state_shift, cache_kv, cache_idx_k, page_table,
           g_pre, w_in, mu_shift, w0, w2, a0, a2, k_k, k_a, r_k, lnx_g, lnx_b, w_out_rwkv, w_out_dsa, w_out,
           g_post, w_ple, w_ple_gate):
    depth = g_pre.shape[0]
    assert depth == 1
    wts = tuple(w[0] for w in (g_pre, w_in, mu_shift, w0, w2, a0, a2, k_k, k_a, r_k, lnx_g, lnx_b,
                               w_out_rwkv, w_out_dsa, w_out, g_post, w_ple, w_ple_gate))
    bp = x_prompt.shape[0]
    yp, kvp, ikp, srp, shp = _layer(
        x_prompt, p_prompt[0], jnp.zeros((bp, SHIFT_W), F32), jnp.zeros((bp, HEADS, HD, HD), F32), None, wts)
    n_phys = cache_kv.shape[1]
    paged = (cache_kv[0].reshape(n_phys, PAGE, 2 * KVW), cache_idx_k[0], page_table)
    ys, kvs, iks, srs, shs = _layer(x_sample, p_sample[0], state_shift[0], state_rwkv[0], paged, wts)
    return (yp, ys, kvp[None], ikp[None], srp[None], shp[None], kvs[None], iks[None], srs[None], shs[None])
```

```python
import functools

import jax
import jax.numpy as jnp
from jax import lax
from jax.experimental import pallas as pl
from jax.experimental.pallas import tpu as pltpu

F32 = jnp.float32
BF16 = jnp.bfloat16
I32 = jnp.int32
HI = lax.Precision.HIGHEST

D_MODEL = 1024
PAGE = 128
HEADS = 8
HD = 64
RW = HEADS * HD
LORA = 64
SHIFT_W = 3 * RW + 2 * LORA
KVH = 2
GROUP = HEADS // KVH
KVW = KVH * HD
IDX_D = 32
IDXW = HEADS * IDX_D
TOPK = 256
QB = 128
PLE = 256
RMS_EPS = 1e-6
GN_EPS = 64e-5
INT_MIN = -2147483648
NEG = -1e30
VMEM_LIMIT = 56 * 1024 * 1024

_O_ZS, _O_ZGR, _O_Q, _O_K, _O_V, _O_QI, _O_KI, _O_WI, _O_ZGA, _O_MR, _O_MA, _O_END = (
    0, 1664, 2176, 2688, 2816, 2944, 3200, 3232, 3240, 3752, 4776, 5800)
QIW_W = 384

NN = (((1,), (0,)), ((), ()))
NT = (((1,), (1,)), ((), ()))
TN = (((0,), (0,)), ((), ()))


def _params(sem):
    return pltpu.CompilerParams(dimension_semantics=sem, vmem_limit_bytes=VMEM_LIMIT)


def _sigmoid(x):
    return 1.0 / (1.0 + jnp.exp(-x))


def _split2(x):
    hi = x.astype(BF16)
    return hi, (x - hi.astype(F32)).astype(BF16)


def _dot3(a, b, dims=NN):
    ah, al = _split2(a)
    bh, bl = _split2(b)
    dg = lambda x, y: lax.dot_general(x, y, dims, preferred_element_type=F32)
    return dg(ah, bh) + (dg(ah, bl) + dg(al, bh))


def _seg_sum(x, ones_bd):
    hi, lo = _split2(x)
    return (jnp.dot(hi, ones_bd, preferred_element_type=F32)
            + jnp.dot(lo, ones_bd, preferred_element_type=F32))


def _in_proj_body(x_ref, g_ref, *refs, n_row, n_col):
    n_out = n_row + n_col
    w_refs, o_refs = refs[:n_out], refs[n_out:]
    x = x_ref[...]
    ms = jnp.mean(x * x, axis=-1, keepdims=True)
    h = ((x * lax.rsqrt(ms + RMS_EPS)) * g_ref[...]).astype(BF16)
    for w_ref, o_ref in zip(w_refs[:n_row], o_refs[:n_row]):
        o_ref[...] = jnp.dot(h, w_ref[...], preferred_element_type=F32).astype(o_ref.dtype)
    for w_ref, o_ref in zip(w_refs[n_row:], o_refs[n_row:]):
        o_ref[0] = lax.dot_general(w_ref[...], h, NT, preferred_element_type=F32).astype(o_ref.dtype)


def _in_proj(x2d, g_pre, row_groups, col_groups, b, tm):
    n = x2d.shape[0]
    nt = max(n // b // tm, 1)
    ws = [w for w, _ in row_groups + col_groups]
    in_specs = [pl.BlockSpec((tm, D_MODEL), lambda i: (i, 0)),
                pl.BlockSpec((1, D_MODEL), lambda i: (0, 0))]
    in_specs += [pl.BlockSpec(w.shape, lambda i: (0, 0), pipeline_mode=pl.Buffered(1)) for w in ws]
    out_specs = [pl.BlockSpec((tm, w.shape[1]), lambda i: (i, 0)) for w, _ in row_groups]
    out_specs += [pl.BlockSpec((1, w.shape[0], tm), lambda i: (i // nt, 0, i % nt)) for w, _ in col_groups]
    out_shape = [jax.ShapeDtypeStruct((n, w.shape[1]), dt) for w, dt in row_groups]
    out_shape += [jax.ShapeDtypeStruct((b, w.shape[0], n // b), dt) for w, dt in col_groups]
    return pl.pallas_call(
        functools.partial(_in_proj_body, n_row=len(row_groups), n_col=len(col_groups)),
        grid=(n // tm,), in_specs=in_specs, out_specs=out_specs, out_shape=out_shape,
        compiler_params=_params(("arbitrary",)), name="in_proj",
    )(x2d, g_pre.reshape(1, D_MODEL), *ws)


def _in_proj_weights(w_in, prompt):
    wb = w_in.astype(BF16)
    pad = jnp.zeros((D_MODEL, QIW_W - (IDXW + IDX_D + HEADS)), BF16)
    qiw = jnp.concatenate([wb[:, _O_QI:_O_KI], wb[:, _O_KI:_O_WI], wb[:, _O_WI:_O_ZGA], pad], axis=1)
    rows = [(wb[:, _O_ZS:_O_ZGR], F32), (wb[:, _O_ZGR:_O_Q], F32), (wb[:, _O_Q:_O_K], F32), (qiw, F32),
            (wb[:, _O_ZGA:_O_MR], F32), (wb[:, _O_MR:_O_MA], F32), (wb[:, _O_MA:_O_END], F32)]
    if not prompt:
        return rows + [(wb[:, _O_K:_O_QI], F32)], []
    rows += [(wb[:, _O_K:_O_V], BF16), (jnp.tile(wb[:, _O_KI:_O_WI], (1, HEADS)), BF16)]
    cols = [(wb[:, _O_K:_O_QI].T, F32), (wb[:, _O_V:_O_QI].T, BF16), (wb[:, _O_KI:_O_WI].T, F32)]
    return rows, cols


def _rwkv_prep_body(zs_ref, sh_ref, mu_ref, w0_ref, w2_ref, a0_ref, a2_ref, kk_ref, ka_ref, ones_ref,
                    r_o, lw_o, k_o, v_o, kk_o, kka_o, prev_scr, *, tm, t_valid, t_total):
    t = pl.program_id(1)

    @pl.when(t == 0)
    def _():
        prev_scr[...] = sh_ref[0]

    zs = zs_ref[0]
    row = lax.broadcasted_iota(I32, (tm, 1), 0)
    prev = jnp.where(row == 0, prev_scr[...], pltpu.roll(zs, 1, 0))
    prev_scr[...] = zs[tm - 1:tm, :]
    xs = zs + (prev - zs) * mu_ref[...]
    r = xs[:, 0:RW]
    k = xs[:, RW:2 * RW]
    v = xs[:, 2 * RW:3 * RW]
    zw = xs[:, 3 * RW:3 * RW + LORA]
    za = xs[:, 3 * RW + LORA:SHIFT_W]
    y = -(w0_ref[...] + jnp.dot(jnp.tanh(zw), w2_ref[...], precision=HI, preferred_element_type=F32))
    softplus = jnp.maximum(y, 0.0) + jnp.log(1.0 + jnp.exp(-jnp.abs(y)))
    lw = -jnp.exp(-softplus - 0.5)
    a = _sigmoid(a0_ref[...] + jnp.dot(za, a2_ref[...], precision=HI, preferred_element_type=F32))
    kk = k * kk_ref[...]
    norm = jnp.sqrt(_seg_sum(kk * kk, ones_ref[...]))
    kk = kk / jnp.maximum(norm, 1e-12)
    k2 = k * (1.0 + (a - 1.0) * ka_ref[...])
    kka = kk * a
    if t_valid < t_total:
        ok = (t * tm + row) < t_valid
        lw = jnp.where(ok, lw, 0.0)
        k2 = jnp.where(ok, k2, 0.0)
        v = jnp.where(ok, v, 0.0)
        kk = jnp.where(ok, kk, 0.0)
        kka = jnp.where(ok, kka, 0.0)
    r_o[0] = r
    lw_o[0] = lw
    k_o[0] = k2
    v_o[0] = v
    kk_o[0] = kk
    kka_o[0] = kka


def _rwkv_prep(zs, shift0, mu, w0, w2, a0, a2, k_k, k_a, ones_bd, tm, t_valid):
    b, t_total, _ = zs.shape
    row = lambda a: a.reshape(1, -1)
    vec_spec = lambda n: pl.BlockSpec((1, n), lambda i, j: (0, 0))
    in_specs = [pl.BlockSpec((1, tm, SHIFT_W), lambda i, j: (i, j, 0)),
                pl.BlockSpec((1, 1, SHIFT_W), lambda i, j: (i, 0, 0)),
                vec_spec(SHIFT_W), vec_spec(RW), pl.BlockSpec((LORA, RW), lambda i, j: (0, 0)),
                vec_spec(RW), pl.BlockSpec((LORA, RW), lambda i, j: (0, 0)), vec_spec(RW), vec_spec(RW),
                pl.BlockSpec((RW, RW), lambda i, j: (0, 0))]
    out_specs = [pl.BlockSpec((1, tm, RW), lambda i, j: (i, j, 0))] * 6
    out_shape = [jax.ShapeDtypeStruct((b, t_total, RW), F32)] * 6
    return pl.pallas_call(
        functools.partial(_rwkv_prep_body, tm=tm, t_valid=t_valid, t_total=t_total),
        grid=(b, t_total // tm), in_specs=in_specs, out_specs=out_specs, out_shape=out_shape,
        scratch_shapes=[pltpu.VMEM((1, SHIFT_W), F32)],
        compiler_params=_params(("arbitrary", "arbitrary")), name="rwkv_prep",
    )(zs, shift0.reshape(b, 1, SHIFT_W), row(mu), row(w0), w2, row(a0), a2, row(k_k), row(k_a), ones_bd)


def _gn_bonus(y, r, k, v, rk, g, bias, ones_bd):
    mean = _seg_sum(y, ones_bd) * (1.0 / HD)
    d = y - mean
    var = _seg_sum(d * d, ones_bd) * (1.0 / HD)
    yn = d * lax.rsqrt(var + GN_EPS) * g + bias
    return yn + _seg_sum(r * k * rk, ones_bd) * v


def _rwkv_chunk_body(r_ref, lw_ref, k_ref, v_ref, kk_ref, kka_ref, s0_ref, rk_ref, g_ref, b_ref, ones_ref,
                     y_ref, st_ref, s_scr, y_scr, *, c):
    t = pl.program_id(1)

    @pl.when(t == 0)
    def _():
        s_scr[...] = s0_ref[0]

    r = r_ref[0]
    lw = lw_ref[0]
    k = k_ref[0]
    v = v_ref[0]
    kk = kk_ref[0]
    kka = kka_ref[0]
    ri = lax.broadcasted_iota(I32, (2 * c, c), 0)
    ci = lax.broadcasted_iota(I32, (2 * c, c), 1)
    rc = lax.broadcasted_iota(I32, (c, c), 0)
    cc = lax.broadcasted_iota(I32, (c, c), 1)
    incl = (rc >= cc).astype(F32).astype(BF16)
    eye_c = (rc == cc).astype(F32)
    mask = (jnp.where(ri < c, ri, ri - c + 1) > ci).astype(F32)
    rh = lax.broadcasted_iota(I32, (HD, HD), 0)
    ch = lax.broadcasted_iota(I32, (HD, HD), 1)
    eye_h = (rh == ch).astype(F32)

    lw_hi, lw_lo = _split2(lw)
    lw_lo2 = (lw - lw_hi.astype(F32) - lw_lo.astype(F32)).astype(BF16)
    bdot = functools.partial(jnp.dot, preferred_element_type=F32)
    cum = bdot(incl, lw_hi) + (bdot(incl, lw_lo) + bdot(incl, lw_lo2))
    tot = cum[c - 1:c, :]
    x_all = jnp.concatenate([kk * jnp.exp(cum - lw), r * jnp.exp(cum)], axis=0)
    inv = jnp.exp(-cum)
    y_all = jnp.concatenate([kka * inv, k * inv], axis=0)
    rem = jnp.exp(tot - cum)
    gh_all = jnp.concatenate([kka * rem, k * rem], axis=0)
    gc = jnp.exp(tot)

    heads = range(HEADS)
    sls = [slice(h * HD, (h + 1) * HD) for h in heads]
    xs = [x_all[:, sl] for sl in sls]
    vs = [v[:, sl] for sl in sls]
    a = [_dot3(xs[h], y_all[:, sls[h]], NT) for h in heads]
    a_l = [a[h][:, :c] * mask for h in heads]
    av = [_dot3(a[h][:, c:] * mask, vs[h]) for h in heads]
    pw = [-a_l[h][:c] for h in heads]
    tinv = [eye_c + pw[h] for h in heads]
    for _ in range(max(c.bit_length() - 2, 0)):
        pw = [_dot3(pw[h], pw[h]) for h in heads]
        tinv = [tinv[h] + _dot3(tinv[h], pw[h]) for h in heads]
    up = [-_dot3(tinv[h], jnp.concatenate([av[h][:c], xs[h][:c]], axis=1)) for h in heads]
    yq = [jnp.concatenate([av[h][c:], xs[h][c:]], axis=1) + _dot3(a_l[h][c:], up[h]) for h in heads]
    gh = [gh_all[:, sl] for sl in sls]
    n_mat = [_dot3(gh[h], jnp.concatenate([up[h][:, :HD], vs[h]], axis=0), TN) for h in heads]
    m_mat = [eye_h * gc[:, sls[h]] + _dot3(gh[h][:c], up[h][:, HD:], TN) for h in heads]
    for h in heads:
        s0 = s_scr[h]
        y_scr[:, sls[h]] = yq[h][:, :HD] + _dot3(yq[h][:, HD:], s0)
        s_scr[h] = n_mat[h] + _dot3(m_mat[h], s0)

    y_ref[0] = _gn_bonus(y_scr[...], r, k, v, rk_ref[...], g_ref[...], b_ref[...], ones_ref[...])

    @pl.when(t == pl.num_programs(1) - 1)
    def _():
        st_ref[0] = s_scr[...]


def _rwkv_chunk(r, lw, k, v, kk, kka, s0t, r_k, lnx_g, lnx_b, ones_bd, c):
    b, t_total, _ = r.shape
    row = lambda a: a.reshape(1, -1)
    seq = pl.BlockSpec((1, c, RW), lambda i, j: (i, j, 0))
    st = pl.BlockSpec((1, HEADS, HD, HD), lambda i, j: (i, 0, 0, 0))
    vec = pl.BlockSpec((1, RW), lambda i, j: (0, 0))
    return pl.pallas_call(
        functools.partial(_rwkv_chunk_body, c=c),
        grid=(b, t_total // c),
        in_specs=[seq] * 6 + [st, vec, vec, vec, pl.BlockSpec((RW, RW), lambda i, j: (0, 0))],
        out_specs=[seq, st],
        out_shape=[jax.ShapeDtypeStruct((b, t_total, RW), F32), jax.ShapeDtypeStruct((b, HEADS, HD, HD), F32)],
        scratch_shapes=[pltpu.VMEM((HEADS, HD, HD), F32), pltpu.VMEM((c, RW), F32)],
        compiler_params=_params(("arbitrary", "arbitrary")), name="rwkv_chunk",
    )(r, lw, k, v, kk, kka, s0t, row(r_k), row(lnx_g), row(lnx_b), ones_bd)


def _rwkv_steps_body(r_ref, lw_ref, k_ref, v_ref, kk_ref, kka_ref, s0_ref, rk_ref, g_ref, b_ref, ones_ref,
                     y_ref, s_ref, y_scr, *, n_steps, rows):
    r = r_ref[0]
    w = jnp.exp(lw_ref[0])
    k = k_ref[0]
    v = v_ref[0]
    kk = kk_ref[0]
    kka = kka_ref[0]
    rh = lax.broadcasted_iota(I32, (HD, HD), 0)
    ch = lax.broadcasted_iota(I32, (HD, HD), 1)
    eye_h = (rh == ch).astype(F32)
    y_scr[...] = jnp.zeros((rows, RW), F32)
    for h in range(HEADS):
        sl = slice(h * HD, (h + 1) * HD)
        s = s0_ref[0, h]
        for t in range(n_steps):
            row = lambda a: a[t:t + 1, sl]
            s_kk = jnp.sum(s * row(kk), axis=1, keepdims=True)
            vk = jnp.dot(eye_h * row(v), jnp.broadcast_to(row(k), (HD, HD)),
                         precision=HI, preferred_element_type=F32)
            s = s * row(w) - s_kk * row(kka) + vk
            y8 = lax.dot_general(jnp.broadcast_to(row(r), (8, HD)), s, NT, precision=HI,
                                 preferred_element_type=F32)
            y_scr[t:t + 1, sl] = y8[0:1, :]
        s_ref[0, h] = s
    y_ref[0] = _gn_bonus(y_scr[...], r, k, v, rk_ref[...], g_ref[...], b_ref[...], ones_ref[...])


def _rwkv_steps(r, lw, k, v, kk, kka, s0, r_k, lnx_g, lnx_b, ones_bd, n_steps):
    b, rows, _ = r.shape
    row = lambda a: a.reshape(1, -1)
    seq = pl.BlockSpec((1, rows, RW), lambda i: (i, 0, 0))
    st = pl.BlockSpec((1, HEADS, HD, HD), lambda i: (i, 0, 0, 0))
    vec = pl.BlockSpec((1, RW), lambda i: (0, 0))
    return pl.pallas_call(
        functools.partial(_rwkv_steps_body, n_steps=n_steps, rows=rows),
        grid=(b,),
        in_specs=[seq] * 6 + [st, vec, vec, vec, pl.BlockSpec((RW, RW), lambda i: (0, 0))],
        out_specs=[seq, st],
        out_shape=[jax.ShapeDtypeStruct((b, rows, RW), F32), jax.ShapeDtypeStruct((b, HEADS, HD, HD), F32)],
        scratch_shapes=[pltpu.VMEM((rows, RW), F32)],
        compiler_params=_params(("arbitrary",)), name="rwkv_steps",
    )(r, lw, k, v, kk, kka, s0, row(r_k), row(lnx_g), row(lnx_b), ones_bd)


def _score_keys(scores):
    s = jnp.where(scores == 0.0, 0.0, scores)
    bits = pltpu.bitcast(s, I32)
    return bits ^ ((bits >> 31) & 0x7FFFFFFF)


def _fold_lanes(x):
    out = x[:, 0:128]
    for j in range(1, x.shape[1] // 128):
        out = out + x[:, j * 128:(j + 1) * 128]
    return out


def _topk_threshold(load_keys, n_chunks, rows, kc, n_sel, idx_bits):
    def count(pred):
        def body(cidx, acc):
            keys = load_keys(cidx)
            col = cidx * kc + lax.broadcasted_iota(I32, (rows, kc), 1)
            return acc + _fold_lanes(jnp.where(pred(keys, col), 1, 0).astype(I32))
        acc = lax.fori_loop(0, n_chunks, body, jnp.zeros((rows, 128), I32))
        return jnp.sum(acc, axis=1, keepdims=True)

    nonneg = count(lambda keys, col: keys >= 0)
    base = jnp.where(nonneg >= n_sel, 0, INT_MIN).astype(I32) * jnp.ones((rows, 1), I32)

    def value_bit(i, base):
        trial = base | jnp.left_shift(jnp.int32(1), 30 - i)
        cnt = count(lambda keys, col: keys >= trial)
        return jnp.where(cnt >= n_sel, trial, base)

    tau = lax.fori_loop(0, 31, value_bit, base)
    need = n_sel - count(lambda keys, col: keys > tau)

    def index_bit(i, x):
        trial = x | jnp.left_shift(jnp.int32(1), idx_bits - 1 - i)
        cnt = count(lambda keys, col: (keys == tau) & (col < trial))
        return jnp.where(cnt <= need, trial, x)

    x = lax.fori_loop(0, idx_bits, index_bit, jnp.zeros((rows, 1), I32))
    return tau, x


def _pad_heads(q, scale):
    rows = q.shape[0]
    lane = lax.broadcasted_iota(I32, (rows, 128), 1)
    out = []
    for h in range(HEADS):
        slab = q[:, (h // 2) * 128:(h // 2 + 1) * 128] * scale
        kvh = h // GROUP
        if (h % 2) != kvh:
            slab = pltpu.roll(slab, HD, 1)
        out.append(jnp.where((lane // HD) == kvh, slab, 0.0))
    return jnp.concatenate(out, axis=0).astype(BF16)


def _unpad_heads(o, rows):
    lane = lax.broadcasted_iota(I32, (rows, 128), 1)
    slabs = []
    for p in range(HEADS // 2):
        parts = []
        for h in (2 * p, 2 * p + 1):
            oh = o[h * rows:(h + 1) * rows, :]
            if (h % 2) != (h // GROUP):
                oh = pltpu.roll(oh, HD, 1)
            parts.append(oh)
        slabs.append(jnp.where(lane < HD, parts[0], parts[1]))
    return slabs


def _attend_chunk(qp, kc, vc, bias, m_scr, l_scr, acc_scr, rows, transposed):
    n = bias.shape[1]
    if transposed:
        s = jnp.dot(qp, kc, preferred_element_type=F32)
    else:
        s = lax.dot_general(qp, kc, NT, preferred_element_type=F32)
    s = (s.reshape(HEADS, rows, n) + bias[None]).reshape(HEADS * rows, n)
    m_old = m_scr[...]
    m_new = jnp.maximum(m_old, jnp.max(s, axis=1, keepdims=True))
    alpha = jnp.exp(m_old - m_new)
    p = jnp.exp(s - m_new)
    l_scr[...] = alpha * l_scr[...] + jnp.sum(p, axis=1, keepdims=True)
    if transposed:
        pv = lax.dot_general(p.astype(BF16), vc, NT, preferred_element_type=F32)
    else:
        pv = jnp.dot(p.astype(BF16), vc, preferred_element_type=F32)
    acc_scr[...] = alpha * acc_scr[...] + pv
    m_scr[...] = m_new


def _count_rows(load_keys, n_chunks, kc, pred):
    def body(c, acc):
        keys = load_keys(c)
        pos = c * kc + lax.broadcasted_iota(I32, (kc, QB), 0)
        ind = jnp.where(pred(keys, pos), 1, 0).astype(I32)
        return acc + jnp.sum(ind.reshape(kc // 8, 8, QB), axis=0)
    acc = lax.fori_loop(0, n_chunks, body, jnp.zeros((8, QB), I32))
    return jnp.sum(acc, axis=0, keepdims=True)


def _dsa_prompt_body(q_ref, qiw_ref, kirep_ref, kb_ref, vt_ref, o_ref, keys_scr, acc_scr, *, kc, n_sel, idx_bits):
    i = pl.program_id(1)
    t0 = i * QB
    n_chunks = (t0 + QB + kc - 1) // kc

    qiw = qiw_ref[0]
    qit = qiw[:, 0:IDXW].T
    wt = qiw[:, IDXW:QIW_W].T[IDX_D:IDX_D + HEADS, :] * (float(IDXW) ** -0.5)
    rowi = lax.broadcasted_iota(I32, (IDXW, QB), 0)
    qmt = jnp.concatenate([jnp.where((rowi // IDX_D) == h, qit, 0.0) for h in range(HEADS)], axis=1).astype(BF16)
    qpos = t0 + lax.broadcasted_iota(I32, (kc, QB), 1)
    chunk = lambda c: pl.ds(pl.multiple_of(c * kc, kc), kc)

    def score_chunk(c, carry):
        d = jnp.dot(kirep_ref[0, chunk(c), :], qmt, preferred_element_type=F32)
        sc = jnp.zeros((kc, QB), F32)
        for h in range(HEADS):
            sc = sc + jnp.maximum(d[:, h * QB:(h + 1) * QB], 0.0) * wt[h:h + 1, :]
        kpos = c * kc + lax.broadcasted_iota(I32, (kc, QB), 0)
        keys_scr[chunk(c), :] = jnp.where(kpos <= qpos, _score_keys(sc), INT_MIN)
        return carry

    lax.fori_loop(0, n_chunks, score_chunk, 0)

    load_keys = lambda c: keys_scr[chunk(c), :]
    count = functools.partial(_count_rows, load_keys, n_chunks, kc)
    base = jnp.where(count(lambda keys, pos: keys >= 0) >= n_sel, 0, INT_MIN).astype(I32)

    def value_bit(b, base):
        trial = base | jnp.left_shift(jnp.int32(1), 30 - b)
        return jnp.where(count(lambda keys, pos: keys >= trial) >= n_sel, trial, base)

    tau = lax.fori_loop(0, 31, value_bit, base)
    tau = jnp.maximum(tau, INT_MIN + 1)
    excess = count(lambda keys, pos: keys >= tau) - n_sel

    @pl.when(jnp.max(excess) > 0)
    def _():
        need = n_sel - count(lambda keys, pos: keys > tau)

        def index_bit(b, x):
            trial = x | jnp.left_shift(jnp.int32(1), idx_bits - 1 - b)
            cnt = count(lambda keys, pos: jnp.where(keys == tau, pos, trial) < trial)
            return jnp.where(cnt <= need, trial, x)

        x = lax.fori_loop(0, idx_bits, index_bit, jnp.zeros((1, QB), I32))

        def drop_ties(c, carry):
            keys = keys_scr[chunk(c), :]
            pos = c * kc + lax.broadcasted_iota(I32, (kc, QB), 0)
            keys_scr[chunk(c), :] = jnp.where(keys == tau, jnp.where(pos >= x, INT_MIN, keys), keys)
            return carry

        lax.fori_loop(0, n_chunks, drop_ties, 0)

    qt = q_ref[0].T * (float(HD) ** -0.5)
    zero = jnp.zeros((HD, QB), F32)
    tiles = []
    for h in range(HEADS):
        blk = qt[h * HD:(h + 1) * HD, :]
        tiles.append(jnp.concatenate([blk, zero] if h // GROUP == 0 else [zero, blk], axis=0))
    qpt = jnp.concatenate(tiles, axis=1).astype(BF16)
    acc_scr[...] = jnp.zeros((128, HEADS * QB), F32)

    def attend_chunk(c, carry):
        m_old, l_old = carry
        bias = jnp.where(keys_scr[chunk(c), :] >= tau, 0.0, NEG)
        s = jnp.dot(kb_ref[0, chunk(c), :], qpt, preferred_element_type=F32)
        s = s + jnp.concatenate([bias] * HEADS, axis=1)
        m_new = jnp.maximum(m_old, jnp.max(s, axis=0, keepdims=True))
        alpha = jnp.exp(m_old - m_new)
        p = jnp.exp(s - m_new)
        pv = jnp.dot(vt_ref[0, :, chunk(c)], p.astype(BF16), preferred_element_type=F32)
        acc_scr[...] = alpha * acc_scr[...] + pv
        return m_new, alpha * l_old + jnp.sum(p, axis=0, keepdims=True)

    init = (jnp.full((1, HEADS * QB), NEG, F32), jnp.zeros((1, HEADS * QB), F32))
    _, l_fin = lax.fori_loop(0, n_chunks, attend_chunk, init)
    o = acc_scr[...] / l_fin
    for p in range(HEADS // 2):
        pair = [o[(h // GROUP) * HD:(h // GROUP + 1) * HD, h * QB:(h + 1) * QB] for h in (2 * p, 2 * p + 1)]
        o_ref[0, :, p * 128:(p + 1) * 128] = jnp.concatenate(pair, axis=0).T


def _dsa_prompt(q, qiw, kirep, kb, vt, kc):
    b, t_total, _ = q.shape
    n_sel = min(TOPK, t_total // 4)
    kc = min(kc, t_total)
    idx_bits = max(t_total.bit_length(), 1)
    return pl.pallas_call(
        functools.partial(_dsa_prompt_body, kc=kc, n_sel=n_sel, idx_bits=idx_bits),
        grid=(b, t_total // QB),
        in_specs=[pl.BlockSpec((1, QB, RW), lambda i, j: (i, j, 0)),
                  pl.BlockSpec((1, QB, QIW_W), lambda i, j: (i, j, 0)),
                  pl.BlockSpec((1, t_total, IDXW), lambda i, j: (i, 0, 0)),
                  pl.BlockSpec((1, t_total, KVW), lambda i, j: (i, 0, 0)),
                  pl.BlockSpec((1, KVW, t_total), lambda i, j: (i, 0, 0))],
        out_specs=pl.BlockSpec((1, QB, RW), lambda i, j: (i, j, 0)),
        out_shape=jax.ShapeDtypeStruct((b, t_total, RW), F32),
        scratch_shapes=[pltpu.VMEM((t_total, QB), I32), pltpu.VMEM((128, HEADS * QB), F32)],
        compiler_params=_params(("arbitrary", "arbitrary")), name="dsa_prompt",
    )(q, qiw, kirep, kb, vt)


def _out_proj_body(x_ref, yr_ref, zgr_ref, ya_ref, zga_ref, mr_ref, ma_ref, p_ref,
                   wr_ref, wa_ref, wo_ref, gp_ref, wple_ref, wg_ref, o_ref):
    silu = lambda z: z * _sigmoid(z)
    bdot = lambda a, w_ref: jnp.dot(a.astype(BF16), w_ref[...], preferred_element_type=F32)
    yr = yr_ref[...] * silu(zgr_ref[...])
    ya = ya_ref[...] * silu(zga_ref[...])
    merged = _sigmoid(mr_ref[...]) * bdot(yr, wr_ref) + _sigmoid(ma_ref[...]) * bdot(ya, wa_ref)
    o1 = bdot(merged, wo_ref)
    ms = jnp.mean(o1 * o1, axis=-1, keepdims=True)
    x2 = x_ref[...] + (o1 * lax.rsqrt(ms + RMS_EPS)) * gp_ref[...]
    o_ref[...] = x2 + bdot(p_ref[...], wple_ref) * _sigmoid(bdot(x2, wg_ref))


def _out_proj(x2d, yr, zgr, ya, zga, mr, ma, p2d, w_r, w_a, w_o, g_post, w_ple, w_gate, tm):
    n = x2d.shape[0]
    rows = lambda w: pl.BlockSpec((tm, w), lambda i: (i, 0))
    full = lambda a: pl.BlockSpec(a.shape, lambda i: (0, 0))
    ws = [w_r.astype(BF16), w_a.astype(BF16), w_o.astype(BF16), g_post.reshape(1, D_MODEL),
          w_ple.astype(BF16), w_gate.astype(BF16)]
    return pl.pallas_call(
        _out_proj_body, grid=(n // tm,),
        in_specs=[rows(D_MODEL), rows(RW), rows(RW), rows(RW), rows(RW), rows(D_MODEL), rows(D_MODEL), rows(PLE)]
        + [full(w) for w in ws],
        out_specs=rows(D_MODEL), out_shape=jax.ShapeDtypeStruct((n, D_MODEL), F32),
        compiler_params=_params(("arbitrary",)), name="out_proj",
    )(x2d, yr, zgr, ya, zga, mr, ma, p2d, *ws)


def _sample_queries(qiw):
    qi = qiw[:, 0:IDXW]
    qs = jnp.concatenate([qi[:, h * IDX_D:(h + 1) * IDX_D] for h in range(HEADS)], axis=0).astype(BF16)
    wi = qiw[:, IDXW + IDX_D:IDXW + IDX_D + HEADS] * (float(IDXW) ** -0.5)
    return qs, wi


def _index_scores(d, wi, rows):
    sc = jnp.zeros((rows, d.shape[1]), F32)
    for h in range(HEADS):
        sc = sc + jnp.maximum(d[h * rows:(h + 1) * rows, :], 0.0) * wi[:, h:h + 1]
    return sc


def _dsa_scores_body(pt_ref, qiw_ref, *refs, pp, rows, n_new):
    page_refs, (kp_ref, kn_ref) = refs[:pp], refs[pp:]
    j = pl.program_id(1)
    qiw = qiw_ref[0]
    qs, wi = _sample_queries(qiw)
    for p in range(pp):
        d = jnp.dot(qs, page_refs[p][0].astype(BF16), preferred_element_type=F32)
        kp_ref[0, :, p * PAGE:(p + 1) * PAGE] = _score_keys(_index_scores(d, wi, rows))

    @pl.when(j == 0)
    def _():
        ki_new = qiw[:, IDXW:IDXW + IDX_D].astype(BF16)
        ki_pad = jnp.concatenate([ki_new, jnp.zeros((PAGE - rows, IDX_D), BF16)], axis=0)
        sc = _index_scores(lax.dot_general(qs, ki_pad, NT, preferred_element_type=F32), wi, rows)
        qrow = lax.broadcasted_iota(I32, (rows, PAGE), 0)
        col = lax.broadcasted_iota(I32, (rows, PAGE), 1)
        kn_ref[0] = jnp.where((col <= qrow) & (col < n_new), _score_keys(sc), INT_MIN)


def _dsa_scores(page_table, qiw, cache_idx, pp, n_new):
    b, rows, _ = qiw.shape
    n_pages = page_table.shape[1]
    page_spec = lambda p: pl.BlockSpec((1, IDX_D, PAGE), lambda i, j, pt: (pt[i, j * pp + p], 0, 0))
    grid_spec = pltpu.PrefetchScalarGridSpec(
        num_scalar_prefetch=1, grid=(b, n_pages // pp),
        in_specs=[pl.BlockSpec((1, rows, QIW_W), lambda i, j, pt: (i, 0, 0))] + [page_spec(p) for p in range(pp)],
        out_specs=[pl.BlockSpec((1, rows, pp * PAGE), lambda i, j, pt: (i, 0, j)),
                   pl.BlockSpec((1, rows, PAGE), lambda i, j, pt: (i, 0, 0))])
    return pl.pallas_call(
        functools.partial(_dsa_scores_body, pp=pp, rows=rows, n_new=n_new),
        grid_spec=grid_spec,
        out_shape=[jax.ShapeDtypeStruct((b, rows, n_pages * PAGE), I32), jax.ShapeDtypeStruct((b, rows, PAGE), I32)],
        compiler_params=_params(("arbitrary", "arbitrary")), name="dsa_scores",
    )(page_table, qiw, *([cache_idx] * pp))


def _dsa_sample_body(pt_ref, q_ref, kp_ref, kn_ref, kvn_ref, *refs, pp, rows, n_new, past, kc, l_pad, n_sel,
                     idx_bits):
    page_refs = refs[:pp]
    o_ref, keys_scr, tau_scr, x_scr, m_scr, l_scr, acc_scr = refs[pp:]
    j = pl.program_id(1)

    @pl.when(j == 0)
    def _():
        keys_scr[:, 0:past] = kp_ref[0]
        keys_scr[:, past:past + PAGE] = kn_ref[0]
        if l_pad > past + PAGE:
            keys_scr[:, past + PAGE:l_pad] = jnp.full((rows, l_pad - past - PAGE), INT_MIN, I32)
        load_keys = lambda c: keys_scr[:, pl.ds(pl.multiple_of(c * kc, kc), kc)]
        tau, x = _topk_threshold(load_keys, l_pad // kc, rows, kc, n_sel, idx_bits)
        tau_scr[...] = tau
        x_scr[...] = x
        m_scr[...] = jnp.full((HEADS * rows, 1), NEG, F32)
        l_scr[...] = jnp.zeros((HEADS * rows, 1), F32)
        acc_scr[...] = jnp.zeros((HEADS * rows, 128), F32)

    tau = tau_scr[...]
    x = x_scr[...]
    qp = _pad_heads(q_ref[0], float(HD) ** -0.5)

    def bias_for(off, extra=None):
        keys = keys_scr[:, pl.ds(off, PAGE)]
        col = off + lax.broadcasted_iota(I32, (rows, PAGE), 1)
        sel = (keys > tau) | ((keys == tau) & (col < x))
        if extra is not None:
            sel = sel & extra
        return jnp.where(sel, 0.0, NEG)

    for p in range(pp):
        off = pl.multiple_of((j * pp + p) * PAGE, PAGE)
        kvp = page_refs[p][0]
        _attend_chunk(qp, kvp[0:KVW, :].astype(BF16), kvp[KVW:2 * KVW, :].astype(BF16), bias_for(off),
                      m_scr, l_scr, acc_scr, rows, True)

    @pl.when(j == pl.num_programs(1) - 1)
    def _():
        kvn = jnp.concatenate([kvn_ref[0], jnp.zeros((PAGE - rows, 2 * KVW), F32)], axis=0)
        qrow = lax.broadcasted_iota(I32, (rows, PAGE), 0)
        col = lax.broadcasted_iota(I32, (rows, PAGE), 1)
        _attend_chunk(qp, kvn[:, 0:128].astype(BF16), kvn[:, 128:256].astype(BF16),
                      bias_for(past, (col <= qrow) & (col < n_new)), m_scr, l_scr, acc_scr, rows, False)
        o = acc_scr[...] / l_scr[...]
        for p, slab in enumerate(_unpad_heads(o, rows)):
            o_ref[0, :, p * 128:(p + 1) * 128] = slab


def _dsa_sample(page_table, q, keys_past, keys_new, kv_new, cache_kv2, pp, n_new, kc):
    b, rows, _ = q.shape
    n_pages = page_table.shape[1]
    past = n_pages * PAGE
    n_sel = min(TOPK, (past + n_new) // 4)
    l_pad = -(-(past + PAGE) // kc) * kc
    idx_bits = l_pad.bit_length()
    page_spec = lambda p: pl.BlockSpec((1, 2 * KVW, PAGE), lambda i, j, pt: (pt[i, j * pp + p], 0, 0))
    per_b = lambda w: pl.BlockSpec((1, rows, w), lambda i, j, pt: (i, 0, 0))
    grid_spec = pltpu.PrefetchScalarGridSpec(
        num_scalar_prefetch=1, grid=(b, n_pages // pp),
        in_specs=[per_b(RW), per_b(past), per_b(PAGE), per_b(2 * KVW)] + [page_spec(p) for p in range(pp)],
        out_specs=per_b(RW),
        scratch_shapes=[pltpu.VMEM((rows, l_pad), I32), pltpu.VMEM((rows, 1), I32), pltpu.VMEM((rows, 1), I32),
                        pltpu.VMEM((HEADS * rows, 1), F32), pltpu.VMEM((HEADS * rows, 1), F32),
                        pltpu.VMEM((HEADS * rows, 128), F32)])
    return pl.pallas_call(
        functools.partial(_dsa_sample_body, pp=pp, rows=rows, n_new=n_new, past=past, kc=kc, l_pad=l_pad,
                          n_sel=n_sel, idx_bits=idx_bits),
        grid_spec=grid_spec, out_shape=jax.ShapeDtypeStruct((b, rows, RW), F32),
        compiler_params=_params(("arbitrary", "arbitrary")), name="dsa_sample",
    )(page_table, q, keys_past, keys_new, kv_new, *([cache_kv2] * pp))


def _ones_block_diag():
    i = jnp.arange(RW) // HD
    return (i[:, None] == i[None, :]).astype(BF16)


def _pad_rows(a, rows):
    return jnp.pad(a, ((0, 0), (0, rows - a.shape[1]), (0, 0)))


def _layer(x, p, shift0, s0, paged, wts):
    (g_pre, w_in, mu_shift, w0, w2, a0, a2, k_k, k_a, r_k, lnx_g, lnx_b,
     w_out_rwkv, w_out_dsa, w_out, g_post, w_ple, w_ple_gate) = wts
    b, t, _ = x.shape
    n = b * t
    x2d = x.reshape(n, D_MODEL)
    ones_bd = _ones_block_diag()
    row_groups, col_groups = _in_proj_weights(w_in, paged is None)
    tm = min(512, t) if col_groups else min(512, n)
    outs = _in_proj(x2d, g_pre, row_groups, col_groups, b, tm)
    zs, zgr, q, qiw, zga, mr, ma = outs[:7]
    seq = lambda a: a.reshape(b, t, a.shape[-1])
    zs3 = seq(zs)
    rk_flat = r_k.reshape(RW)
    if paged is None:
        kb, kirep, kvt, vt, kit = outs[7:]
        prep = _rwkv_prep(zs3, shift0, mu_shift, w0, w2, a0, a2, k_k, k_a, ones_bd, min(512, t), t)
        c = min(64, t)
        y_r, st = _rwkv_chunk(*prep, jnp.swapaxes(s0, -1, -2), rk_flat, lnx_g, lnx_b, ones_bd, c)
        s_new = jnp.swapaxes(st, -1, -2)
        y_a = _dsa_prompt(seq(q), seq(qiw), seq(kirep), seq(kb), vt, 512)
        kv_out = jnp.transpose(kvt.reshape(b, 2, KVH, HD, t), (0, 4, 1, 2, 3))
        idx_k = jnp.swapaxes(kit, 1, 2)
    else:
        kv = outs[7]
        cache_kvt, cache_idxt, page_table = paged
        rows = 8
        prep = _rwkv_prep(_pad_rows(zs3, rows), shift0, mu_shift, w0, w2, a0, a2, k_k, k_a, ones_bd, rows, t)
        y_r, s_new = _rwkv_steps(*prep, s0, rk_flat, lnx_g, lnx_b, ones_bd, t)
        y_r = y_r[:, :t]
        qiw_p = _pad_rows(seq(qiw), rows)
        keys_past, keys_new = _dsa_scores(page_table, qiw_p, cache_idxt, 8, t)
        y_a = _dsa_sample(page_table, _pad_rows(seq(q), rows), keys_past, keys_new, _pad_rows(seq(kv), rows),
                          cache_kvt, 8, t, 2048)[:, :t]
        kv_out = kv.reshape(b, t, 2, KVH, HD)
        idx_k = seq(qiw)[:, :, IDXW:IDXW + IDX_D]
    y = _out_proj(x2d, y_r.reshape(n, RW), zgr, y_a.reshape(n, RW), zga, mr, ma, p.reshape(n, PLE),
                  w_out_rwkv, w_out_dsa, w_out, g_post, w_ple, w_ple_gate, min(512, n))
    return y.reshape(b, t, D_MODEL), kv_out, idx_k, s_new, zs3[:, -1]


def kernel(x_prompt, x_sample, p_prompt, p_sample, state_rwkv, state_shift, cache_kv, cache_idx_k, page_table,
           g_pre, w_in, mu_shift, w0, w2, a0, a2, k_k, k_a, r_k, lnx_g, lnx_b, w_out_rwkv, w_out_dsa, w_out,
           g_post, w_ple, w_ple_gate):
    depth = g_pre.shape[0]
    assert depth == 1
    wts = tuple(w[0] for w in (g_pre, w_in, mu_shift, w0, w2, a0, a2, k_k, k_a, r_k, lnx_g, lnx_b,
                               w_out_rwkv, w_out_dsa, w_out, g_post, w_ple, w_ple_gate))
    bp = x_prompt.shape[0]
    yp, kvp, ikp, srp, shp = _layer(
        x_prompt, p_prompt[0], jnp.zeros((bp, SHIFT_W), F32), jnp.zeros((bp, HEADS, HD, HD), F32), None, wts)
    n_phys = cache_kv.shape[1]
    paged = (jnp.transpose(cache_kv[0], (0, 2, 3, 4, 1)).reshape(n_phys, 2 * KVW, PAGE),
             jnp.swapaxes(cache_idx_k[0], 1, 2), page_table)
    ys, kvs, iks, srs, shs = _layer(x_sample, p_sample[0], state_shift[0], state_rwkv[0], paged, wts)
    return (yp, ys, kvp[None], ikp[None], srp[None], shp[None], kvs[None], iks[None], srs[None], shs[None])
```
